```python
import jax, jax.numpy as jnp
from jax import lax
import numpy as np


D_MODEL = 1024
BATCH = 2
SEQ = 16384
DEPTH = 2

HEAD_DIM = 64
N_HEADS = D_MODEL // HEAD_DIM
ML_HEADS = N_HEADS // 4
FOX_HEADS = (N_HEADS - ML_HEADS) // 2
MOBA_HEADS = N_HEADS - ML_HEADS - FOX_HEADS
FOX_W = FOX_HEADS * HEAD_DIM
ML_W = ML_HEADS * HEAD_DIM
MOBA_W = MOBA_HEADS * HEAD_DIM
Q_BLOCK = 128
ML_CHUNK = 64
ML_CONV = 4
MOBA_BLOCK = 256
MOBA_TOPK = 3
ALIBI_MAX = 8.0
D_FF = ((8 * D_MODEL // 3 + 255) // 256) * 256
EPS = 1e-6
SPLIT_SIZES = (FOX_W, FOX_W, FOX_W, FOX_HEADS,
               ML_W, ML_W, ML_W, ML_HEADS, ML_HEADS, ML_W,
               MOBA_W, MOBA_W, MOBA_W)
IN_COLS = sum(SPLIT_SIZES)

kernel_name = 'hybrid_fox_mlstm_moba_block'


def _rms(x, g=None):
    xf = x.astype(jnp.float32)
    y = xf * lax.rsqrt(jnp.mean(xf * xf, axis=-1, keepdims=True) + EPS)
    return y if g is None else y * g.astype(jnp.float32)


def _to_heads(x, h):
    b, t, _ = x.shape
    return x.reshape(b, t, h, HEAD_DIM).transpose(0, 2, 1, 3)


def _from_heads(x):
    b, h, t, d = x.shape
    return x.transpose(0, 2, 1, 3).reshape(b, t, h * d)


def _alibi_slopes(h):
    return jnp.exp2(-ALIBI_MAX * jnp.arange(1, h + 1, dtype=jnp.float32) / h)


def _causal_conv(x, w, bias):
    ch = x.shape[-1]
    y = lax.conv_general_dilated(x, w.astype(x.dtype)[:, None, :], window_strides=(1,),
                                 padding=((ML_CONV - 1, 0),),
                                 dimension_numbers=('NWC', 'WIO', 'NWC'),
                                 feature_group_count=ch)
    return y + bias.astype(x.dtype)


def _fox_attention(q, k, v, f_pre):
    b, h, t, d = q.shape
    nq = t // Q_BLOCK
    logf = jax.nn.log_sigmoid(f_pre).transpose(0, 2, 1)
    cum = jnp.cumsum(logf, axis=-1)
    kpos = jnp.arange(t)
    qb = q.reshape(b, h, nq, Q_BLOCK, d).transpose(2, 0, 1, 3, 4)
    cq = cum.reshape(b, h, nq, Q_BLOCK).transpose(2, 0, 1, 3)
    starts = jnp.arange(nq) * Q_BLOCK
    scale = HEAD_DIM ** -0.5

    def block(args):
        qc, cqc, st = args
        qpos = st + jnp.arange(Q_BLOCK)
        s = jnp.einsum('bhqd,bhkd->bhqk', qc, k) * scale
        s = s + (cqc[..., None] - cum[:, :, None, :])
        s = jnp.where(kpos[None, :] <= qpos[:, None], s, -jnp.inf)
        p = jax.nn.softmax(s, axis=-1)
        return jnp.einsum('bhqk,bhkd->bhqd', p, v)

    out = lax.map(block, (qb, cq, starts))
    return out.transpose(1, 2, 0, 3, 4).reshape(b, h, t, d)


def _mlstm(q, k, v, i_pre, f_pre):
    b, h, t, d = q.shape
    nc = t // ML_CHUNK
    k = k * (d ** -0.5)
    itil = i_pre.transpose(0, 2, 1).reshape(b, h, nc, ML_CHUNK)
    logf = jax.nn.log_sigmoid(f_pre).transpose(0, 2, 1).reshape(b, h, nc, ML_CHUNK)
    bcum = jnp.cumsum(logf, axis=-1)
    bend = bcum[..., -1]
    wlog = bend[..., None] - bcum + itil
    qc = q.reshape(b, h, nc, ML_CHUNK, d)
    kc = k.reshape(b, h, nc, ML_CHUNK, d)
    vc = v.reshape(b, h, nc, ML_CHUNK, d)

    def step(carry, xs):
        cmat, nvec, m = carry
        kx, vx, wx, bx = xs
        m_new = jnp.maximum(bx + m, wx.max(-1))
        decay = jnp.exp(bx + m - m_new)
        ws = jnp.exp(wx - m_new[..., None])
        c_new = decay[..., None, None] * cmat + jnp.einsum('bhs,bhsd,bhse->bhde', ws, vx, kx)
        n_new = decay[..., None] * nvec + jnp.einsum('bhs,bhse->bhe', ws, kx)
        return (c_new, n_new, m_new), (cmat, nvec, m)

    init = (jnp.zeros((b, h, d, d), jnp.float32), jnp.zeros((b, h, d), jnp.float32),
            jnp.zeros((b, h), jnp.float32))
    xs = (kc.transpose(2, 0, 1, 3, 4), vc.transpose(2, 0, 1, 3, 4),
          wlog.transpose(2, 0, 1, 3), bend.transpose(2, 0, 1))
    _, (cs, ns, ms) = lax.scan(step, init, xs)
    cs = cs.transpose(1, 2, 0, 3, 4)
    ns = ns.transpose(1, 2, 0, 3)
    ms = ms.transpose(1, 2, 0)

    causal = jnp.tril(jnp.ones((ML_CHUNK, ML_CHUNK), dtype=bool))
    logd = bcum[..., :, None] - bcum[..., None, :] + itil[..., None, :]
    logd = jnp.where(causal, logd, -jnp.inf)
    m_inter = bcum + ms[..., None]
    m_j = jnp.maximum(m_inter, logd.max(-1))
    smat = jnp.einsum('bhcjd,bhcsd->bhcjs', qc, kc) * jnp.exp(logd - m_j[..., None])
    inter = jnp.exp(m_inter - m_j)
    num = (inter[..., None] * jnp.einsum('bhcde,bhcje->bhcjd', cs, qc)
           + jnp.einsum('bhcjs,bhcsd->bhcjd', smat, vc))
    den = inter * jnp.einsum('bhce,bhcje->bhcj', ns, qc) + smat.sum(-1)
    hout = num / jnp.maximum(jnp.abs(den), jnp.exp(-m_j))[..., None]
    return hout.reshape(b, h, t, d)


def _moba_attention(q, k, v, slopes):
    b, h, t, d = q.shape
    nb = -(-t // MOBA_BLOCK)
    pad = nb * MOBA_BLOCK - t
    kb = jnp.pad(k, ((0, 0), (0, 0), (0, pad), (0, 0))).reshape(b, h, nb, MOBA_BLOCK, d)
    vb = jnp.pad(v, ((0, 0), (0, 0), (0, pad), (0, 0))).reshape(b, h, nb, MOBA_BLOCK, d)
    kmean = kb.mean(axis=3)
    qblk = jnp.arange(t) // MOBA_BLOCK
    gate = jnp.einsum('bhtd,bhnd->bhtn', q, kmean)
    gate = jnp.where(jnp.arange(nb)[None, :] < qblk[:, None], gate, -jnp.inf)
    topk = min(MOBA_TOPK, nb)
    _, sel = lax.top_k(gate, topk)
    valid = sel < qblk[:, None]
    nq = t // Q_BLOCK
    xs = (q.reshape(b, h, nq, Q_BLOCK, d).transpose(2, 0, 1, 3, 4),
          sel.reshape(b, h, nq, Q_BLOCK, topk).transpose(2, 0, 1, 3, 4),
          valid.reshape(b, h, nq, Q_BLOCK, topk).transpose(2, 0, 1, 3, 4),
          jnp.arange(nq))
    bi = jnp.arange(b)[:, None, None, None]
    hi = jnp.arange(h)[None, :, None, None]
    scale = HEAD_DIM ** -0.5
    offs = jnp.arange(MOBA_BLOCK)

    def block(args):
        qc, selc, validc, ci = args
        qpos = ci * Q_BLOCK + jnp.arange(Q_BLOCK)
        ks = kb[bi, hi, selc]
        vs = vb[bi, hi, selc]
        kpos_sel = selc[..., None] * MOBA_BLOCK + offs
        dist_sel = (qpos[:, None, None] - kpos_sel).astype(jnp.float32)
        s_sel = (jnp.einsum('bhqd,bhqkld->bhqkl', qc, ks) * scale
                 - slopes[:, None, None, None] * dist_sel)
        s_sel = jnp.where(validc[..., None], s_sel, -jnp.inf)
        own = (ci * Q_BLOCK) // MOBA_BLOCK
        ko = lax.dynamic_index_in_dim(kb, own, axis=2, keepdims=False)
        vo = lax.dynamic_index_in_dim(vb, own, axis=2, keepdims=False)
        dist_own = qpos[:, None] - (own * MOBA_BLOCK + offs)[None, :]
        s_own = (jnp.einsum('bhqd,bhld->bhql', qc, ko) * scale
                 - slopes[:, None, None] * dist_own.astype(jnp.float32))
        s_own = jnp.where(dist_own >= 0, s_own, -jnp.inf)
        s_all = jnp.concatenate([s_sel.reshape(b, h, Q_BLOCK, topk * MOBA_BLOCK), s_own], axis=-1)
        p = jax.nn.softmax(s_all, axis=-1)
        p_sel = p[..., :topk * MOBA_BLOCK].reshape(b, h, Q_BLOCK, topk, MOBA_BLOCK)
        p_own = p[..., topk * MOBA_BLOCK:]
        return (jnp.einsum('bhqkl,bhqkld->bhqd', p_sel, vs)
                + jnp.einsum('bhql,bhld->bhqd', p_own, vo))

    out = lax.map(block, xs)
    return out.transpose(1, 2, 0, 3, 4).reshape(b, h, t, d)


def _mixer(hn, w_in, fox_f_bias, fox_q_g, fox_k_g, ml_conv_w, ml_conv_b, ml_i_bias,
           ml_f_bias, ml_h_g, moba_q_g, moba_k_g):
    proj = (hn @ w_in).astype(jnp.float32)
    idx = [int(i) for i in np.cumsum(SPLIT_SIZES)[:-1]]
    fq, fk, fv, ff, mq, mk, mv, mi, mf, mo, bq, bk, bv = jnp.split(proj, idx, axis=-1)
    y_fox = _fox_attention(_rms(_to_heads(fq, FOX_HEADS), fox_q_g),
                           _rms(_to_heads(fk, FOX_HEADS), fox_k_g),
                           _to_heads(fv, FOX_HEADS),
                           ff + fox_f_bias.astype(jnp.float32))
    qk = jax.nn.silu(_causal_conv(jnp.concatenate([mq, mk], axis=-1), ml_conv_w, ml_conv_b))
    mq, mk = jnp.split(qk, 2, axis=-1)
    h_ml = _mlstm(_to_heads(mq, ML_HEADS), _to_heads(mk, ML_HEADS), _to_heads(mv, ML_HEADS),
                  mi + ml_i_bias.astype(jnp.float32), mf + ml_f_bias.astype(jnp.float32))
    y_ml = _from_heads(_rms(h_ml)) * ml_h_g.astype(jnp.float32) * jax.nn.sigmoid(mo)
    y_moba = _moba_attention(_rms(_to_heads(bq, MOBA_HEADS), moba_q_g),
                             _rms(_to_heads(bk, MOBA_HEADS), moba_k_g),
                             _to_heads(bv, MOBA_HEADS), _alibi_slopes(MOBA_HEADS))
    return jnp.concatenate([_from_heads(y_fox), y_ml, _from_heads(y_moba)], axis=-1)


def setup_inputs(seed: int = 0) -> dict:
    key = jax.random.key(seed)
    ks = jax.random.split(key, 22)
    n = jax.random.normal
    f32 = jnp.float32
    return {
        'x': n(ks[0], (BATCH, SEQ, D_MODEL), f32),
        'c': n(ks[1], (BATCH, D_MODEL), f32),
        'w_ada': n(ks[2], (DEPTH, D_MODEL, 6 * D_MODEL), f32) * (0.5 * D_MODEL ** -0.5),
        'b_ada': n(ks[3], (DEPTH, 6 * D_MODEL), f32) * 0.01,
        'norm1_g': 1.0 + 0.02 * n(ks[4], (DEPTH, D_MODEL), f32),
        'norm2_g': 1.0 + 0.02 * n(ks[5], (DEPTH, D_MODEL), f32),
        'w_in': n(ks[6], (DEPTH, D_MODEL, IN_COLS), f32) * D_MODEL ** -0.5,
        'fox_f_bias': 2.0 + 0.1 * n(ks[7], (DEPTH, FOX_HEADS), f32),
        'fox_q_g': 1.0 + 0.02 * n(ks[8], (DEPTH, HEAD_DIM), f32),
        'fox_k_g': 1.0 + 0.02 * n(ks[9], (DEPTH, HEAD_DIM), f32),
        'ml_conv_w': n(ks[10], (DEPTH, ML_CONV, 2 * ML_W), f32) * ML_CONV ** -0.5,
        'ml_conv_b': 0.01 * n(ks[11], (DEPTH, 2 * ML_W), f32),
        'ml_i_bias': 0.1 * n(ks[12], (DEPTH, ML_HEADS), f32),
        'ml_f_bias': 3.0 + 0.1 * n(ks[13], (DEPTH, ML_HEADS), f32),
        'ml_h_g': 1.0 + 0.02 * n(ks[14], (DEPTH, ML_W), f32),
        'moba_q_g': 1.0 + 0.02 * n(ks[15], (DEPTH, HEAD_DIM), f32),
        'moba_k_g': 1.0 + 0.02 * n(ks[16], (DEPTH, HEAD_DIM), f32),
        'w_out': n(ks[17], (DEPTH, D_MODEL, D_MODEL), f32) * D_MODEL ** -0.5,
        'w_gate_up': n(ks[18], (DEPTH, D_MODEL, 2 * D_FF), f32) * D_MODEL ** -0.5,
        'w_down': n(ks[19], (DEPTH, D_FF, D_MODEL), f32) * D_FF ** -0.5,
    }


def reference(x, c, w_ada, b_ada, norm1_g, norm2_g, w_in, fox_f_bias, fox_q_g, fox_k_g,
              ml_conv_w, ml_conv_b, ml_i_bias, ml_f_bias, ml_h_g, moba_q_g, moba_k_g,
              w_out, w_gate_up, w_down):
    for l in range(DEPTH):
        mod = jax.nn.silu(c) @ w_ada[l] + b_ada[l]
        sh1, sc1, g1, sh2, sc2, g2 = jnp.split(mod[:, None, :], 6, axis=-1)
        hn = (_rms(x, norm1_g[l]) * (1 + sc1) + sh1).astype(x.dtype)
        y = _mixer(hn, w_in[l], fox_f_bias[l], fox_q_g[l], fox_k_g[l], ml_conv_w[l],
                   ml_conv_b[l], ml_i_bias[l], ml_f_bias[l], ml_h_g[l], moba_q_g[l], moba_k_g[l])
        x = x + g1 * (y.astype(x.dtype) @ w_out[l])
        hn = (_rms(x, norm2_g[l]) * (1 + sc2) + sh2).astype(x.dtype)
        gate, up = jnp.split(hn @ w_gate_up[l], 2, axis=-1)
        x = x + g2 * ((jax.nn.silu(gate) * up) @ w_down[l])
    return x
```

```python
import functools

import jax
import jax.numpy as jnp
import numpy as np
from jax import lax
from jax.experimental import pallas as pl
from jax.experimental.pallas import tpu as pltpu

F32 = jnp.float32
BF16 = jnp.bfloat16

HEAD_DIM = 64
PAIR_W = 2 * HEAD_DIM
FOX_HEADS = 6
ML_HEADS = 4
MOBA_HEADS = 6
FOX_W = FOX_HEADS * HEAD_DIM
ML_W = ML_HEADS * HEAD_DIM
MOBA_W = MOBA_HEADS * HEAD_DIM
ML_CONV = 4
MOBA_BLOCK = 256
MOBA_TOPK = 3
ALIBI_MAX = 8.0
EPS = 1e-6
NEG = -1e30
QK_SCALE = HEAD_DIM ** -0.5

GATE_W = 128
GATE_FOX_F = 0
GATE_ML_I = FOX_HEADS
GATE_ML_F = FOX_HEADS + ML_HEADS

VMEM_LIMIT = 56 * 1024 * 1024

RM_FK, RM_BK, RM_MQ, RM_MK, RM_MV, RM_MO, RM_G = 0, 384, 768, 1024, 1280, 1536, 1792
RM_COLS = 1920
T_FQ, T_FV, T_BQ, T_BV = 0, 384, 768, 1152
T_ROWS = 1536


def _split3(x):
    hi = x.astype(BF16)
    r1 = x - hi.astype(F32)
    mid = r1.astype(BF16)
    lo = (r1 - mid.astype(F32)).astype(BF16)
    return hi, mid, lo


def _tri_cumsum(x, tri):
    hi, mid, lo = _split3(x)
    cat = jnp.concatenate([hi, mid, lo], axis=1)
    cs = jnp.dot(tri, cat, preferred_element_type=F32)
    w = x.shape[1]
    return cs[:, :w] + cs[:, w:2 * w] + cs[:, 2 * w:]


def _lower_tri(n):
    r = lax.broadcasted_iota(jnp.int32, (n, n), 0)
    c = lax.broadcasted_iota(jnp.int32, (n, n), 1)
    return jnp.where(c <= r, 1.0, 0.0).astype(BF16)


def _log_sigmoid(x):
    return jnp.minimum(x, 0.0) - jnp.log1p(jnp.exp(-jnp.abs(x)))


def _sigmoid(x):
    return 1.0 / (1.0 + jnp.exp(-x))


def _ada_kernel(c_ref, w_ref, b_ref, o_ref):
    c = c_ref[...]
    s = c * _sigmoid(c)
    hi, mid, lo = _split3(s)
    w1, w2, w3 = _split3(w_ref[...])
    acc = jnp.dot(hi, w1, preferred_element_type=F32)
    acc += jnp.dot(hi, w2, preferred_element_type=F32)
    acc += jnp.dot(mid, w1, preferred_element_type=F32)
    acc += jnp.dot(hi, w3, preferred_element_type=F32)
    acc += jnp.dot(mid, w2, preferred_element_type=F32)
    acc += jnp.dot(lo, w1, preferred_element_type=F32)
    o_ref[...] = acc + b_ref[...]


def _ada_mod(c, w_ada, b_ada):
    depth, d, six_d = w_ada.shape
    b = c.shape[0]
    rows = 8
    cp = jnp.zeros((rows, d), F32).at[:b].set(c)
    blk = 1024
    out = pl.pallas_call(
        _ada_kernel,
        grid=(depth, six_d // blk),
        in_specs=[
            pl.BlockSpec((rows, d), lambda l, j: (0, 0)),
            pl.BlockSpec((None, d, blk), lambda l, j: (l, 0, j)),
            pl.BlockSpec((None, 1, blk), lambda l, j: (l, 0, j)),
        ],
        out_specs=pl.BlockSpec((None, rows, blk), lambda l, j: (l, 0, j)),
        out_shape=jax.ShapeDtypeStruct((depth, rows, six_d), F32),
        compiler_params=pltpu.CompilerParams(
            dimension_semantics=("arbitrary", "arbitrary"), vmem_limit_bytes=VMEM_LIMIT),
        name="ada_mod",
    )(cp, w_ada, b_ada.reshape(depth, 1, six_d))
    return out[:, :b].reshape(depth, b, 6, d)


def _pair_rms(xp, grow):
    lane = lax.broadcasted_iota(jnp.int32, xp.shape, 1)
    lo = lane < HEAD_DIM
    x2 = xp * xp
    s_lo = jnp.sum(jnp.where(lo, x2, 0.0), axis=-1, keepdims=True)
    s_hi = jnp.sum(jnp.where(lo, 0.0, x2), axis=-1, keepdims=True)
    ms = jnp.where(lo, s_lo, s_hi) * (1.0 / HEAD_DIM)
    return xp * lax.rsqrt(ms + EPS) * grow


def _proj_kernel(tm, tiles_per_batch,
                 x_ref, mod_ref, g1_ref, wrm_ref, wt_ref, kg_ref, qg_ref, gb_ref, cw_ref, cb_ref,
                 tout_ref, krm_ref, mlqkv_ref, mo_ref, gates_ref, kmean_ref,
                 conv_buf, cum_carry):
    i = pl.program_id(0)
    first = (i % tiles_per_batch) == 0

    x = x_ref[...]
    ms = jnp.mean(x * x, axis=-1, keepdims=True)
    mod = mod_ref[...]
    hn = (x * lax.rsqrt(ms + EPS) * g1_ref[...]) * (1.0 + mod[1:2, :]) + mod[0:1, :]
    hb = hn.astype(BF16)

    rm = jnp.dot(hb, wrm_ref[...], preferred_element_type=F32)
    tt = lax.dot_general(wt_ref[...], hb, (((1,), (1,)), ((), ())),
                         preferred_element_type=F32)

    qg = qg_ref[...]
    for base, gbase in ((T_FQ, 0), (T_BQ, FOX_W)):
        for h in range(FOX_HEADS):
            r0 = base + h * HEAD_DIM
            blk = tt[r0:r0 + HEAD_DIM, :]
            msq = jnp.mean(blk * blk, axis=0, keepdims=True)
            g = qg[gbase + h * HEAD_DIM:gbase + (h + 1) * HEAD_DIM, :]
            tout_ref[r0:r0 + HEAD_DIM, :] = (blk * lax.rsqrt(msq + EPS) * g).astype(BF16)
    tout_ref[T_FV:T_FV + FOX_W, :] = tt[T_FV:T_FV + FOX_W, :].astype(BF16)
    tout_ref[T_BV:T_BV + MOBA_W, :] = tt[T_BV:T_BV + MOBA_W, :].astype(BF16)

    kg = kg_ref[...]
    for p in range(6):
        c0 = p * PAIR_W
        kn = _pair_rms(rm[:, c0:c0 + PAIR_W], kg[:, c0:c0 + PAIR_W])
        krm_ref[:, c0:c0 + PAIR_W] = kn.astype(BF16)
        if p >= 3:
            for a in range(tm // MOBA_BLOCK):
                mean = jnp.mean(kn[a * MOBA_BLOCK:(a + 1) * MOBA_BLOCK, :], axis=0, keepdims=True)
                kmean_ref[a, :, c0 - MOBA_W:c0 - MOBA_W + PAIR_W] = mean

    @pl.when(first)
    def _():
        conv_buf[0:8, :] = jnp.zeros((8, 2 * ML_W), F32)

    @pl.when(jnp.logical_not(first))
    def _():
        conv_buf[0:8, :] = conv_buf[tm:tm + 8, :]

    conv_buf[8:tm + 8, :] = rm[:, RM_MQ:RM_MQ + 2 * ML_W]
    cw = cw_ref[...]
    y = jnp.broadcast_to(cb_ref[...], (tm, 2 * ML_W))
    for j in range(ML_CONV):
        y = y + cw[j:j + 1, :] * conv_buf[pl.ds(8 - (ML_CONV - 1) + j, tm), :]
    y = y * _sigmoid(y)
    mlqkv_ref[:, 0:ML_W] = y[:, 0:ML_W].astype(BF16)
    mlqkv_ref[:, ML_W:2 * ML_W] = (y[:, ML_W:] * QK_SCALE).astype(BF16)
    mlqkv_ref[:, 2 * ML_W:3 * ML_W] = rm[:, RM_MV:RM_MV + ML_W].astype(BF16)
    mo_ref[...] = rm[:, RM_MO:RM_MO + ML_W]

    g = rm[:, RM_G:RM_G + GATE_W] + gb_ref[...]
    lane = lax.broadcasted_iota(jnp.int32, g.shape, 1)
    is_f = (lane < GATE_ML_I) | ((lane >= GATE_ML_F) & (lane < GATE_ML_F + ML_HEADS))
    vals = jnp.where(is_f, _log_sigmoid(g), g)

    @pl.when(first)
    def _():
        cum_carry[...] = jnp.zeros_like(cum_carry)

    cum = _tri_cumsum(vals, _lower_tri(tm)) + cum_carry[...]
    cum_carry[...] = cum[tm - 1:tm, :]
    gates_ref[...] = jnp.where(lane < GATE_ML_I, cum, vals)


def _projection(x2d, mod_l, g1, wrm, wt, kg_row, qg_col, gbias, conv_w, conv_b, seq, tm):
    n, d = x2d.shape
    tiles_per_batch = seq // tm
    nb_tile = tm // MOBA_BLOCK
    const = lambda i: (0, 0)
    return pl.pallas_call(
        functools.partial(_proj_kernel, tm, tiles_per_batch),
        grid=(n // tm,),
        in_specs=[
            pl.BlockSpec((tm, d), lambda i: (i, 0)),
            pl.BlockSpec((None, 6, d), lambda i: (i // tiles_per_batch, 0, 0)),
            pl.BlockSpec((1, d), const),
            pl.BlockSpec((d, RM_COLS), const),
            pl.BlockSpec((T_ROWS, d), const),
            pl.BlockSpec((1, 2 * FOX_W), const),
            pl.BlockSpec((2 * FOX_W, 1), const),
            pl.BlockSpec((1, GATE_W), const),
            pl.BlockSpec((ML_CONV, 2 * ML_W), const),
            pl.BlockSpec((1, 2 * ML_W), const),
        ],
        out_specs=[
            pl.BlockSpec((T_ROWS, tm), lambda i: (0, i)),
            pl.BlockSpec((tm, 2 * FOX_W), lambda i: (i, 0)),
            pl.BlockSpec((tm, 3 * ML_W), lambda i: (i, 0)),
            pl.BlockSpec((tm, ML_W), lambda i: (i, 0)),
            pl.BlockSpec((tm, GATE_W), lambda i: (i, 0)),
            pl.BlockSpec((nb_tile, 1, MOBA_W), lambda i: (i, 0, 0)),
        ],
        out_shape=[
            jax.ShapeDtypeStruct((T_ROWS, n), BF16),
            jax.ShapeDtypeStruct((n, 2 * FOX_W), BF16),
            jax.ShapeDtypeStruct((n, 3 * ML_W), BF16),
            jax.ShapeDtypeStruct((n, ML_W), F32),
            jax.ShapeDtypeStruct((n, GATE_W), F32),
            jax.ShapeDtypeStruct((n // MOBA_BLOCK, 1, MOBA_W), F32),
        ],
        scratch_shapes=[
            pltpu.VMEM((tm + 8, 2 * ML_W), F32),
            pltpu.VMEM((1, GATE_W), F32),
        ],
        compiler_params=pltpu.CompilerParams(
            dimension_semantics=("arbitrary",), vmem_limit_bytes=VMEM_LIMIT),
        name="projection",
    )(x2d, mod_l, g1, wrm, wt, kg_row, qg_col, gbias, conv_w, conv_b)


def _masked_q(q_ref, qm_ref, n_heads):
    row = lax.broadcasted_iota(jnp.int32, (PAIR_W, q_ref.shape[1]), 0)
    for h in range(n_heads):
        p = h // 2
        qp = q_ref[p * PAIR_W:(p + 1) * PAIR_W, :]
        keep = (row < HEAD_DIM) if h % 2 == 0 else (row >= HEAD_DIM)
        qm_ref[h] = jnp.where(keep, qp, jnp.zeros_like(qp))


def _softmax_step(h, s_t, v_ref, m_ref, l_ref, acc_ref):
    m_old = m_ref[h:h + 1, :]
    m_new = jnp.maximum(m_old, jnp.max(s_t, axis=0, keepdims=True))
    alpha = jnp.exp(m_old - m_new)
    p = jnp.exp(s_t - m_new)
    l_ref[h:h + 1, :] = alpha * l_ref[h:h + 1, :] + jnp.sum(p, axis=0, keepdims=True)
    r0 = h * HEAD_DIM
    pv = jnp.dot(v_ref[r0:r0 + HEAD_DIM, :], p.astype(BF16), preferred_element_type=F32)
    acc_ref[r0:r0 + HEAD_DIM, :] = alpha * acc_ref[r0:r0 + HEAD_DIM, :] + pv
    m_ref[h:h + 1, :] = m_new


def _attn_init(m_ref, l_ref, acc_ref):
    m_ref[...] = jnp.full(m_ref.shape, NEG, F32)
    l_ref[...] = jnp.zeros(l_ref.shape, F32)
    acc_ref[...] = jnp.zeros(acc_ref.shape, F32)


def _attn_finish(n_heads, o_ref, l_ref, acc_ref):
    for h in range(n_heads):
        r0 = h * HEAD_DIM
        inv = 1.0 / l_ref[h:h + 1, :]
        o_ref[r0:r0 + HEAD_DIM, :] = (acc_ref[r0:r0 + HEAD_DIM, :] * inv).astype(o_ref.dtype)


def _fox_kernel(tq, tk, it_ref, jt_ref, q_ref, k_ref, v_ref, cum_ref, o_ref,
                qm_ref, m_ref, l_ref, acc_ref):
    s = pl.program_id(1)
    qi = it_ref[s]
    kj = jt_ref[s]
    delta = qi * tq - kj * tk
    j_last = ((qi + 1) * tq - 1) // tk

    @pl.when(kj == 0)
    def _():
        _attn_init(m_ref, l_ref, acc_ref)
        _masked_q(q_ref, qm_ref, FOX_HEADS)

    def step(masked):
        if masked:
            r = lax.broadcasted_iota(jnp.int32, (tk, tq), 0)
            c = lax.broadcasted_iota(jnp.int32, (tk, tq), 1)
            visible = (r - c) <= delta
        for h in range(FOX_HEADS):
            p = h // 2
            s_t = jnp.dot(k_ref[:, p * PAIR_W:(p + 1) * PAIR_W], qm_ref[h],
                          preferred_element_type=F32)
            s_t = s_t - cum_ref[:, GATE_FOX_F + h:GATE_FOX_F + h + 1]
            if masked:
                s_t = jnp.where(visible, s_t, NEG)
            _softmax_step(h, s_t, v_ref, m_ref, l_ref, acc_ref)

    on_diag = (kj + 1) * tk - 1 > qi * tq

    @pl.when(on_diag)
    def _():
        step(True)

    @pl.when(jnp.logical_not(on_diag))
    def _():
        step(False)

    @pl.when(kj == j_last)
    def _():
        _attn_finish(FOX_HEADS, o_ref, l_ref, acc_ref)


def _causal_tables(nq, tq, tk):
    it, jt = [], []
    for i in range(nq):
        for j in range(((i + 1) * tq - 1) // tk + 1):
            it.append(i)
            jt.append(j)
    return jnp.asarray(it, jnp.int32), jnp.asarray(jt, jnp.int32)


def _fox_attention(tout, krm, gates, batch, seq, tq, tk):
    nq, nk = seq // tq, seq // tk
    it, jt = _causal_tables(nq, tq, tk)
    n = batch * seq
    grid_spec = pltpu.PrefetchScalarGridSpec(
        num_scalar_prefetch=2,
        grid=(batch, it.shape[0]),
        in_specs=[
            pl.BlockSpec((FOX_W, tq), lambda b, s, it, jt: (T_FQ // FOX_W, b * nq + it[s])),
            pl.BlockSpec((tk, FOX_W), lambda b, s, it, jt: (b * nk + jt[s], RM_FK // FOX_W)),
            pl.BlockSpec((FOX_W, tk), lambda b, s, it, jt: (T_FV // FOX_W, b * nk + jt[s])),
            pl.BlockSpec((tk, GATE_W), lambda b, s, it, jt: (b * nk + jt[s], 0)),
        ],
        out_specs=pl.BlockSpec((FOX_W, tq), lambda b, s, it, jt: (0, b * nq + it[s])),
        scratch_shapes=[
            pltpu.VMEM((FOX_HEADS, PAIR_W, tq), BF16),
            pltpu.VMEM((8, tq), F32),
            pltpu.VMEM((8, tq), F32),
            pltpu.VMEM((FOX_W, tq), F32),
        ],
    )
    return pl.pallas_call(
        functools.partial(_fox_kernel, tq, tk),
        grid_spec=grid_spec,
        out_shape=jax.ShapeDtypeStruct((FOX_W, n), BF16),
        compiler_params=pltpu.CompilerParams(
            dimension_semantics=("arbitrary", "arbitrary"), vmem_limit_bytes=VMEM_LIMIT),
        name="fox_attention",
    )(it, jt, tout, krm, tout, gates)


def _moba_gate_kernel(tq, nb, q_ref, km_ref, sel_ref, qm_ref):
    qi = pl.program_id(1)
    _masked_q(q_ref, qm_ref, MOBA_HEADS)
    blk = lax.broadcasted_iota(jnp.int32, (nb, tq), 0).astype(F32)
    tpos = qi * tq + lax.broadcasted_iota(jnp.int32, (nb, tq), 1)
    qblk = lax.shift_right_logical(tpos, int(np.log2(MOBA_BLOCK))).astype(F32)
    eligible = blk < qblk
    km = km_ref[...].astype(BF16)
    for h in range(MOBA_HEADS):
        p = h // 2
        gate = jnp.dot(km[:, p * PAIR_W:(p + 1) * PAIR_W], qm_ref[h], preferred_element_type=F32)
        gate = jnp.where(eligible, gate, NEG)
        chosen = blk == qblk
        for _ in range(MOBA_TOPK):
            mx = jnp.max(gate, axis=0, keepdims=True)
            first = jnp.min(jnp.where(gate == mx, blk, float(nb)), axis=0, keepdims=True)
            pick = (blk == first) & (mx > NEG)
            chosen = chosen | pick
            gate = jnp.where(pick, NEG, gate)
        sel_ref[h] = jnp.where(chosen, 0.0, NEG)
    for h in range(MOBA_HEADS, sel_ref.shape[0]):
        sel_ref[h] = jnp.full((nb, tq), NEG, F32)


def _moba_gate(tout, kmean, batch, seq, tq):
    nq = seq // tq
    nb = seq // MOBA_BLOCK
    return pl.pallas_call(
        functools.partial(_moba_gate_kernel, tq, nb),
        grid=(batch, nq),
        in_specs=[
            pl.BlockSpec((MOBA_W, tq), lambda b, i: (T_BQ // MOBA_W, b * nq + i)),
            pl.BlockSpec((None, nb, MOBA_W), lambda b, i: (b, 0, 0)),
        ],
        out_specs=pl.BlockSpec((None, 8, nb, tq), lambda b, i: (b, 0, 0, i)),
        out_shape=jax.ShapeDtypeStruct((batch, 8, nb, seq), F32),
        scratch_shapes=[pltpu.VMEM((MOBA_HEADS, PAIR_W, tq), BF16)],
        compiler_params=pltpu.CompilerParams(
            dimension_semantics=("arbitrary", "arbitrary"), vmem_limit_bytes=VMEM_LIMIT),
        name="moba_gate",
    )(tout, kmean)


def _alibi_slope(h):
    return float(2.0 ** (-ALIBI_MAX * (h + 1) / MOBA_HEADS))


def _moba_kernel(tq, tk, it_ref, jt_ref, q_ref, k_ref, v_ref, sel_ref, o_ref,
                 qm_ref, m_ref, l_ref, acc_ref):
    s = pl.program_id(1)
    qi = it_ref[s]
    kj = jt_ref[s]
    delta = qi * tq - kj * tk
    j_last = ((qi + 1) * tq - 1) // tk
    sub = tk // MOBA_BLOCK
    row0 = (kj * sub) % 8

    @pl.when(kj == 0)
    def _():
        _attn_init(m_ref, l_ref, acc_ref)
        _masked_q(q_ref, qm_ref, MOBA_HEADS)

    def step(masked):
        r = lax.broadcasted_iota(jnp.int32, (tk, tq), 0)
        rel = (lax.broadcasted_iota(jnp.int32, (tk, 1), 0) - delta).astype(F32)
        if masked:
            c = lax.broadcasted_iota(jnp.int32, (tk, tq), 1)
            visible = (r - c) <= delta
        for h in range(MOBA_HEADS):
            p = h // 2
            s_t = jnp.dot(k_ref[:, p * PAIR_W:(p + 1) * PAIR_W], qm_ref[h],
                          preferred_element_type=F32)
            s_t = s_t + _alibi_slope(h) * rel
            sel = sel_ref[h, pl.ds(row0, 1), :]
            for a in range(1, sub):
                sel = jnp.where(r < a * MOBA_BLOCK, sel, sel_ref[h, pl.ds(row0 + a, 1), :])
            s_t = s_t + sel
            if masked:
                s_t = jnp.where(visible, s_t, NEG)
            _softmax_step(h, s_t, v_ref, m_ref, l_ref, acc_ref)

    on_diag = (kj + 1) * tk - 1 > qi * tq

    @pl.when(on_diag)
    def _():
        step(True)

    @pl.when(jnp.logical_not(on_diag))
    def _():
        step(False)

    @pl.when(kj == j_last)
    def _():
        _attn_finish(MOBA_HEADS, o_ref, l_ref, acc_ref)


def _moba_attention(tout, krm, sel, batch, seq, tq, tk):
    nq, nk = seq // tq, seq // tk
    sub = tk // MOBA_BLOCK
    it, jt = _causal_tables(nq, tq, tk)
    n = batch * seq
    grid_spec = pltpu.PrefetchScalarGridSpec(
        num_scalar_prefetch=2,
        grid=(batch, it.shape[0]),
        in_specs=[
            pl.BlockSpec((MOBA_W, tq), lambda b, s, it, jt: (T_BQ // MOBA_W, b * nq + it[s])),
            pl.BlockSpec((tk, MOBA_W), lambda b, s, it, jt: (b * nk + jt[s], RM_BK // MOBA_W)),
            pl.BlockSpec((MOBA_W, tk), lambda b, s, it, jt: (T_BV // MOBA_W, b * nk + jt[s])),
            pl.BlockSpec((None, 8, 8, tq), lambda b, s, it, jt: (b, 0, (jt[s] * sub) // 8, it[s])),
        ],
        out_specs=pl.BlockSpec((MOBA_W, tq), lambda b, s, it, jt: (0, b * nq + it[s])),
        scratch_shapes=[
            pltpu.VMEM((MOBA_HEADS, PAIR_W, tq), BF16),
            pltpu.VMEM((8, tq), F32),
            pltpu.VMEM((8, tq), F32),
            pltpu.VMEM((MOBA_W, tq), F32),
        ],
    )
    return pl.pallas_call(
        functools.partial(_moba_kernel, tq, tk),
        grid_spec=grid_spec,
        out_shape=jax.ShapeDtypeStruct((MOBA_W, n), BF16),
        compiler_params=pltpu.CompilerParams(
            dimension_semantics=("arbitrary", "arbitrary"), vmem_limit_bytes=VMEM_LIMIT),
        name="moba_attention",
    )(it, jt, tout, krm, tout, sel)


def _mlstm_kernel(lc, qkv_ref, mo_ref, gates_ref, hg_ref, o_ref, ct_ref, nt_ref, m_ref):
    ci = pl.program_id(1)

    @pl.when(ci == 0)
    def _():
        ct_ref[...] = jnp.zeros(ct_ref.shape, F32)
        nt_ref[...] = jnp.zeros(nt_ref.shape, F32)
        m_ref[...] = jnp.zeros(m_ref.shape, F32)

    g = gates_ref[...]
    bc = _tri_cumsum(g, _lower_tri(lc))
    g_t = g.T
    bc_t = bc.T

    rr = lax.broadcasted_iota(jnp.int32, (lc, lc), 0)
    cc = lax.broadcasted_iota(jnp.int32, (lc, lc), 1)
    causal = cc <= rr
    lane = lax.broadcasted_iota(jnp.int32, (lc, PAIR_W), 1)
    lo = lane < HEAD_DIM
    lo_row = lo[0:1, :]
    er = lax.broadcasted_iota(jnp.int32, (PAIR_W, PAIR_W), 0)
    ec = lax.broadcasted_iota(jnp.int32, (PAIR_W, PAIR_W), 1)
    same_head = (er < HEAD_DIM) == (ec < HEAD_DIM)
    ones = jnp.ones((lc, PAIR_W), BF16)

    for pr in range(ML_HEADS // 2):
        c0 = pr * PAIR_W
        q2 = qkv_ref[:, c0:c0 + PAIR_W]
        k2 = qkv_ref[:, ML_W + c0:ML_W + c0 + PAIR_W]
        v2 = qkv_ref[:, 2 * ML_W + c0:2 * ML_W + c0 + PAIR_W]
        num, den, isc, emn, wcol, dec = [], [], [], [], [], []
        for half in range(2):
            h = 2 * pr + half
            li, lf = GATE_ML_I + h, GATE_ML_F + h
            icol = g[:, li:li + 1]
            irow = g_t[li:li + 1, :]
            bcol = bc[:, lf:lf + 1]
            brow = bc_t[lf:lf + 1, :]
            bend = bcol[lc - 1:lc, :]
            m_old = m_ref[h:h + 1, 0:1]

            logd = jnp.where(causal, bcol - brow + irow, NEG)
            m_inter = bcol + m_old
            m_row = jnp.maximum(m_inter, jnp.max(logd, axis=-1, keepdims=True))
            dmat = jnp.exp(logd - m_row)
            keep = lo if half == 0 else jnp.logical_not(lo)
            qh = jnp.where(keep, q2, jnp.zeros_like(q2))
            s = lax.dot_general(qh, k2, (((1,), (1,)), ((), ())), preferred_element_type=F32)
            sm = (s * dmat).astype(BF16)
            num.append(jnp.dot(sm, v2, preferred_element_type=F32))
            den.append(jnp.dot(sm, ones, preferred_element_type=F32))
            isc.append(jnp.exp(m_inter - m_row))
            emn.append(jnp.exp(-m_row))

            wlog = bend - bcol + icol
            m_new = jnp.maximum(bend + m_old, jnp.max(wlog, axis=0, keepdims=True))
            wcol.append(jnp.exp(wlog - m_new))
            dec.append(jnp.exp(bend + m_old - m_new))
            m_ref[h:h + 1, :] = jnp.broadcast_to(m_new, (1, m_ref.shape[1]))

        ct = ct_ref[pr]
        nt = nt_ref[pr]
        isc2 = jnp.where(lo, isc[0], isc[1])
        num2 = jnp.where(lo, num[0], num[1]) + isc2 * jnp.dot(q2, ct.astype(BF16), preferred_element_type=F32)
        den2 = jnp.where(lo, den[0], den[1]) + isc2 * jnp.dot(q2, nt.astype(BF16), preferred_element_type=F32)
        hout = num2 / jnp.maximum(jnp.abs(den2), jnp.where(lo, emn[0], emn[1]))
        y = _pair_rms(hout, hg_ref[:, c0:c0 + PAIR_W]) * _sigmoid(mo_ref[:, c0:c0 + PAIR_W])
        o_ref[:, c0:c0 + PAIR_W] = y.astype(o_ref.dtype)

        w2 = jnp.where(lo, wcol[0], wcol[1])
        dec2 = jnp.where(lo_row, dec[0], dec[1])
        wv = (v2.astype(F32) * w2).astype(BF16)
        tn = (((0,), (0,)), ((), ()))
        ct_upd = lax.dot_general(k2, wv, tn, preferred_element_type=F32)
        nt_upd = lax.dot_general(k2, w2.astype(BF16), tn, preferred_element_type=F32)
        ct_ref[pr] = dec2 * ct + jnp.where(same_head, ct_upd, 0.0)
        nt_ref[pr] = dec2 * nt + jnp.where(same_head, nt_upd, 0.0)


def _mlstm(mlqkv, mo, gates, hg_row, batch, seq, lc):
    nc = seq // lc
    n = batch * seq
    return pl.pallas_call(
        functools.partial(_mlstm_kernel, lc),
        grid=(batch, nc),
        in_specs=[
            pl.BlockSpec((lc, 3 * ML_W), lambda b, c: (b * nc + c, 0)),
            pl.BlockSpec((lc, ML_W), lambda b, c: (b * nc + c, 0)),
            pl.BlockSpec((lc, GATE_W), lambda b, c: (b * nc + c, 0)),
            pl.BlockSpec((1, ML_W), lambda b, c: (0, 0)),
        ],
        out_specs=pl.BlockSpec((lc, ML_W), lambda b, c: (b * nc + c, 0)),
        out_shape=jax.ShapeDtypeStruct((n, ML_W), BF16),
        scratch_shapes=[
            pltpu.VMEM((ML_HEADS // 2, PAIR_W, PAIR_W), F32),
            pltpu.VMEM((ML_HEADS // 2, PAIR_W, PAIR_W), F32),
            pltpu.VMEM((8, 128), F32),
        ],
        compiler_params=pltpu.CompilerParams(
            dimension_semantics=("arbitrary", "arbitrary"), vmem_limit_bytes=VMEM_LIMIT),
        name="mlstm",
    )(mlqkv, mo, gates, hg_row)


def _ffn_kernel(dff, x_ref, mod_ref, g2_ref, yf_ref, ym_ref, yb_ref, wo_ref, wgu_ref, wd_ref, o_ref):
    tn = (((0,), (0,)), ((), ()))
    attn = lax.dot_general(yf_ref[...], wo_ref[0:FOX_W, :], tn, preferred_element_type=F32)
    attn += jnp.dot(ym_ref[...], wo_ref[FOX_W:FOX_W + ML_W, :], preferred_element_type=F32)
    attn += lax.dot_general(yb_ref[...], wo_ref[FOX_W + ML_W:, :], tn, preferred_element_type=F32)
    mod = mod_ref[...]
    x1 = x_ref[...] + mod[2:3, :] * attn
    ms = jnp.mean(x1 * x1, axis=-1, keepdims=True)
    hn = ((x1 * lax.rsqrt(ms + EPS) * g2_ref[...]) * (1.0 + mod[4:5, :]) + mod[3:4, :]).astype(BF16)
    half = dff // 2
    acc = None
    for c in range(2):
        gate = jnp.dot(hn, wgu_ref[:, c * half:(c + 1) * half], preferred_element_type=F32)
        up = jnp.dot(hn, wgu_ref[:, dff + c * half:dff + (c + 1) * half], preferred_element_type=F32)
        a = (gate * _sigmoid(gate) * up).astype(BF16)
        part = jnp.dot(a, wd_ref[c * half:(c + 1) * half, :], preferred_element_type=F32)
        acc = part if acc is None else acc + part
    o_ref[...] = x1 + mod[5:6, :] * acc


def _out_ffn(x2d, mod_l, g2, y_fox_t, y_ml, y_moba_t, wo, wgu, wd, seq, tm):
    n, d = x2d.shape
    dff = wd.shape[0]
    tiles_per_batch = seq // tm
    const = lambda i: (0, 0)
    resident = dict(pipeline_mode=pl.Buffered(1))
    return pl.pallas_call(
        functools.partial(_ffn_kernel, dff),
        grid=(n // tm,),
        in_specs=[
            pl.BlockSpec((tm, d), lambda i: (i, 0)),
            pl.BlockSpec((None, 6, d), lambda i: (i // tiles_per_batch, 0, 0)),
            pl.BlockSpec((1, d), const),
            pl.BlockSpec((FOX_W, tm), lambda i: (0, i)),
            pl.BlockSpec((tm, ML_W), lambda i: (i, 0)),
            pl.BlockSpec((MOBA_W, tm), lambda i: (0, i)),
            pl.BlockSpec((d, d), const, **resident),
            pl.BlockSpec((d, 2 * dff), const, **resident),
            pl.BlockSpec((dff, d), const, **resident),
        ],
        out_specs=pl.BlockSpec((tm, d), lambda i: (i, 0)),
        out_shape=jax.ShapeDtypeStruct((n, d), F32),
        compiler_params=pltpu.CompilerParams(
            dimension_semantics=("arbitrary",), vmem_limit_bytes=VMEM_LIMIT),
        name="out_ffn",
    )(x2d, mod_l, g2, y_fox_t, y_ml, y_moba_t, wo, wgu, wd)


def _layer_weights(w_in_l):
    sizes = (FOX_W, FOX_W, FOX_W, FOX_HEADS, ML_W, ML_W, ML_W, ML_HEADS, ML_HEADS, ML_W,
             MOBA_W, MOBA_W, MOBA_W)
    offs = np.cumsum((0,) + sizes)
    fq, fk, fv, ff, mq, mk, mv, mi, mf, mo, bq, bk, bv = (
        w_in_l[:, offs[j]:offs[j + 1]] for j in range(len(sizes)))
    d = w_in_l.shape[0]
    gate_pad = jnp.zeros((d, GATE_W - FOX_HEADS - 2 * ML_HEADS), w_in_l.dtype)
    wrm = jnp.concatenate([fk, bk, mq, mk, mv, mo, ff, mi, mf, gate_pad], axis=1).astype(BF16)
    wt = jnp.concatenate([fq, fv, bq, bv], axis=1).T.astype(BF16)
    return wrm, wt


def kernel(x, c, w_ada, b_ada, norm1_g, norm2_g, w_in, fox_f_bias, fox_q_g, fox_k_g, ml_conv_w,
           ml_conv_b, ml_i_bias, ml_f_bias, ml_h_g, moba_q_g, moba_k_g, w_out, w_gate_up, w_down):
    batch, seq, d = x.shape
    depth = w_in.shape[0]
    n = batch * seq
    tm_proj = 512
    tm_ffn = 512
    tq, tk = 1024, 512
    lc = 256

    mod = _ada_mod(c, w_ada, b_ada)
    x2d = x.reshape(n, d)
    for l in range(depth):
        wrm, wt = _layer_weights(w_in[l])
        kg_row = jnp.concatenate([jnp.tile(fox_k_g[l], FOX_HEADS), jnp.tile(moba_k_g[l], MOBA_HEADS)])[None, :]
        qg_col = (jnp.concatenate([jnp.tile(fox_q_g[l], FOX_HEADS), jnp.tile(moba_q_g[l], MOBA_HEADS)])
                  * QK_SCALE)[:, None]
        gbias = jnp.concatenate([fox_f_bias[l], ml_i_bias[l], ml_f_bias[l],
                                 jnp.zeros((GATE_W - FOX_HEADS - 2 * ML_HEADS,), F32)])[None, :]
        tout, krm, mlqkv, mo, gates, kmean = _projection(
            x2d, mod[l], norm1_g[l][None, :], wrm, wt, kg_row, qg_col, gbias,
            ml_conv_w[l], ml_conv_b[l][None, :], seq, tm_proj)
        y_fox_t = _fox_attention(tout, krm, gates, batch, seq, tq, tk)
        sel = _moba_gate(tout, kmean.reshape(batch, seq // MOBA_BLOCK, MOBA_W), batch, seq, tq)
        y_moba_t = _moba_attention(tout, krm, sel, batch, seq, tq, tk)
        y_ml = _mlstm(mlqkv, mo, gates, ml_h_g[l][None, :], batch, seq, lc)
        x2d = _out_ffn(x2d, mod[l], norm2_g[l][None, :], y_fox_t, y_ml, y_moba_t,
                       w_out[l].astype(BF16), w_gate_up[l].astype(BF16), w_down[l].astype(BF16),
                       seq, tm_ffn)
    return x2d.reshape(batch, seq, d)
```
